```python
import jax, jax.numpy as jnp
from jax import lax
import numpy as np

D_MODEL = 2048
BATCH = 16
SEQ = 256
DEPTH = 2
DEC_BATCH = 2
DEC_SEQ = 1024
PAST_LEN = 512

GRID_W = 64
HEAD_DIM = 128
N_Q_HEADS = 8
N_KV_HEADS = 2
GQA_GROUP = N_Q_HEADS // N_KV_HEADS
D_ATTN = N_Q_HEADS * HEAD_DIM
D_KV = N_KV_HEADS * HEAD_DIM
D_SG = D_MODEL - D_ATTN
N_SG_GROUPS = 8
SG_GROUP_DIM = D_SG // N_SG_GROUPS
CHUNK = 128
D_IN_AB = D_ATTN + 2 * D_KV + 2 * D_SG
Q_BLOCK = 128
ROPE_THETA = 10000.0
D_RNN = D_MODEL
N_RNN_BLOCKS = 16
RNN_BLOCK = D_RNN // N_RNN_BLOCKS
CONV_W = 4
CONV_LEFT = 2
LRU_C = 8.0
N_EXPERTS = 16
CAPACITY_FACTOR = 2
D_EXPERT = 2048
EPS = 1e-6
N_ATTN_LAYERS = (DEPTH + 1) // 2
N_RNN_LAYERS = DEPTH // 2

kernel_name = 'hybrid_diffusion_attn_gmlp_rglru_ecmoe_step'

F32 = jnp.float32


def rms_norm(x, g):
    x32 = x.astype(F32)
    y = x32 * lax.rsqrt(jnp.mean(x32 * x32, axis=-1, keepdims=True) + EPS)
    return (y * g.astype(F32)).astype(x.dtype)


def modulation(cond, w_mod, b_mod):
    m = jax.nn.silu(cond) @ w_mod + b_mod
    return jnp.split(m[:, None, :], 6, axis=-1)


def modulate(h, shift, scale):
    return h * (1 + scale) + shift


def axial_rope_tables(n):
    n_rows = n // GRID_W
    rows = jnp.repeat(jnp.arange(n_rows), GRID_W).astype(F32)
    cols = jnp.tile(jnp.arange(GRID_W), n_rows).astype(F32)
    half = HEAD_DIM // 2
    inv = ROPE_THETA ** (-jnp.arange(0, half, 2, dtype=F32) / half)
    ang = jnp.concatenate([rows[:, None] * inv, cols[:, None] * inv], axis=-1)
    return jnp.cos(ang), jnp.sin(ang)


def apply_rope(x, cos, sin):
    x32 = x.astype(F32).reshape(*x.shape[:-1], HEAD_DIM // 2, 2)
    x1, x2 = x32[..., 0], x32[..., 1]
    c = cos[None, :, None, :]
    s = sin[None, :, None, :]
    out = jnp.stack([x1 * c - x2 * s, x1 * s + x2 * c], axis=-1).reshape(x.shape)
    return out.astype(x.dtype)


def block_attention(q, k, v):
    b, nq = q.shape[0], q.shape[1]
    nb = nq // Q_BLOCK
    qb = q.reshape(b, nb, Q_BLOCK, N_KV_HEADS, GQA_GROUP, HEAD_DIM).transpose(1, 0, 2, 3, 4, 5)
    k32 = k.astype(F32)
    v32 = v.astype(F32)
    scale = HEAD_DIM ** -0.5

    def one_block(qblk):
        s = jnp.einsum('bqkgd,bskd->bkgqs', qblk.astype(F32), k32) * scale
        p = jax.nn.softmax(s, axis=-1)
        return jnp.einsum('bkgqs,bskd->bqkgd', p, v32)

    o = lax.map(one_block, qb)
    return o.transpose(1, 0, 2, 3, 4, 5).reshape(b, nq, D_ATTN).astype(q.dtype)


def spatial_gating(z, sg_gain, w_s, b_s):
    b, n, _ = z.shape
    u, vg = jnp.split(z, 2, axis=-1)
    vg = rms_norm(vg.reshape(b, n, N_SG_GROUPS, SG_GROUP_DIM), sg_gain)
    vg = vg.reshape(b, n // CHUNK, CHUNK, N_SG_GROUPS, SG_GROUP_DIM)
    mixed = jnp.einsum('gqp,bnpgc->bnqgc', w_s, vg) + b_s.T[None, None, :, :, None]
    return u * mixed.reshape(b, n, D_SG)


def ab_project(h, w_in, q_gain, k_gain):
    b, n, _ = h.shape
    proj = h @ w_in
    q, k, v, z = jnp.split(proj, [D_ATTN, D_ATTN + D_KV, D_ATTN + 2 * D_KV], axis=-1)
    q = rms_norm(q.reshape(b, n, N_Q_HEADS, HEAD_DIM), q_gain)
    k = rms_norm(k.reshape(b, n, N_KV_HEADS, HEAD_DIM), k_gain)
    v = v.reshape(b, n, N_KV_HEADS, HEAD_DIM)
    return q, k, v, jax.nn.gelu(z)


def ab_mixer_context(h, w_in, q_gain, k_gain, sg_gain, w_s, b_s, w_out):
    q, k, v, z = ab_project(h, w_in, q_gain, k_gain)
    attn = block_attention(q, k, v)
    sg = spatial_gating(z, sg_gain, w_s, b_s)
    out = jnp.concatenate([attn, sg], axis=-1) @ w_out
    return out, k, v


def ab_mixer_latent(h, ctx_k, ctx_v, cos, sin, w_in, q_gain, k_gain, sg_gain, w_s, b_s, w_out):
    q, k, v, z = ab_project(h, w_in, q_gain, k_gain)
    q = apply_rope(q, cos, sin)
    k = apply_rope(k, cos, sin)
    keys = jnp.concatenate([ctx_k.astype(k.dtype), k], axis=1)
    vals = jnp.concatenate([ctx_v.astype(v.dtype), v], axis=1)
    attn = block_attention(q, keys, vals)
    sg = spatial_gating(z, sg_gain, w_s, b_s)
    return jnp.concatenate([attn, sg], axis=-1) @ w_out


def centred_dwconv(x, w, b):
    n = x.shape[1]
    xp = jnp.pad(x, ((0, 0), (CONV_LEFT, CONV_W - 1 - CONV_LEFT), (0, 0)))
    acc = xp[:, 0:n] * w[0]
    for j in range(1, CONV_W):
        acc = acc + xp[:, j:j + n] * w[j]
    return acc + b


def lin_combine(e1, e2):
    a1, b1 = e1
    a2, b2 = e2
    return a1 * a2, a2 * b1 + b2


def rglru_mixer(h, h0, w_in, conv_w, conv_b, w_gx, b_gx, w_ga, b_ga, lam, w_out, return_state):
    b, n, _ = h.shape
    xz = h @ w_in
    xb, gb = jnp.split(xz, 2, axis=-1)
    xc = centred_dwconv(xb, conv_w, conv_b).astype(F32)
    xblk = xc.reshape(b, n, N_RNN_BLOCKS, RNN_BLOCK)
    gx = jax.nn.sigmoid(jnp.einsum('bnkc,zkcd->zbnkd', xblk, w_gx.astype(F32)).reshape(2, b, n, D_RNN)
                        + b_gx.astype(F32)[:, None, None, :])
    ga = jax.nn.sigmoid(jnp.einsum('bnkc,zkcd->zbnkd', xblk, w_ga.astype(F32)).reshape(2, b, n, D_RNN)
                        + b_ga.astype(F32)[:, None, None, :])
    log_a = -LRU_C * ga * jax.nn.softplus(-lam.astype(F32))[:, None, None, :]
    a = jnp.exp(log_a)
    inp = jnp.sqrt(1.0 - jnp.exp(2.0 * log_a)) * gx * xc[None]
    h0 = h0.astype(F32)
    bf = inp[0].at[:, 0].add(a[0][:, 0] * h0[:, 0])
    bb = inp[1].at[:, -1].add(a[1][:, -1] * h0[:, 1])
    _, hf = lax.associative_scan(lin_combine, (a[0], bf), axis=1)
    _, hb = lax.associative_scan(lin_combine, (a[1], bb), reverse=True, axis=1)
    y = (hf + hb) * jax.nn.gelu(gb.astype(F32))
    out = y.astype(h.dtype) @ w_out
    if return_state:
        return out, jnp.stack([hf[:, -1], hb[:, 0]], axis=1)
    return out


def expert_choice_moe(h, w_router, w_gate, w_up, w_down):
    b, n, _ = h.shape
    cap = CAPACITY_FACTOR * n // N_EXPERTS
    aff = jax.nn.softmax((h @ w_router).astype(F32), axis=-1)
    gates, idx = lax.top_k(aff.transpose(0, 2, 1), cap)
    bidx = jnp.arange(b)[:, None, None]
    xs = h[bidx, idx]
    hid = jax.nn.silu(jnp.einsum('becd,edf->becf', xs, w_gate)) * jnp.einsum('becd,edf->becf', xs, w_up)
    ys = jnp.einsum('becf,efd->becd', hid, w_down) * gates[..., None].astype(h.dtype)
    return jnp.zeros_like(h).at[bidx, idx].add(ys.astype(h.dtype))


def setup_inputs(seed: int = 0) -> dict:
    key = jax.random.key(seed)
    ks = jax.random.split(key, 32)
    nrm = lambda k, s, sc: jax.random.normal(k, s, F32) * sc
    u = jax.random.uniform(ks[22], (N_RNN_LAYERS, 2, D_RNN), F32, 0.9, 0.999)
    s = u ** (1.0 / LRU_C)
    lam = jnp.log(s) - jnp.log1p(-s)
    return {
        'x_prompt': nrm(ks[0], (BATCH, SEQ, D_MODEL), 1.0),
        'x_sample': nrm(ks[1], (DEC_BATCH, DEC_SEQ, D_MODEL), 1.0),
        'c': nrm(ks[2], (DEC_BATCH, D_MODEL), 1.0),
        'cache_k': nrm(ks[3], (DEC_BATCH, N_ATTN_LAYERS, PAST_LEN, N_KV_HEADS, HEAD_DIM), 1.0),
        'cache_v': nrm(ks[4], (DEC_BATCH, N_ATTN_LAYERS, PAST_LEN, N_KV_HEADS, HEAD_DIM), 1.0),
        'state_rglru': nrm(ks[5], (DEC_BATCH, N_RNN_LAYERS, 2, D_RNN), 0.5),
        'c_ctx': nrm(ks[6], (D_MODEL,), 1.0),
        'w_mod': nrm(ks[7], (DEPTH, D_MODEL, 6 * D_MODEL), 0.5 * D_MODEL ** -0.5),
        'b_mod': nrm(ks[8], (DEPTH, 6 * D_MODEL), 0.01),
        'norm_mix': 1.0 + nrm(ks[9], (DEPTH, D_MODEL), 0.02),
        'norm_ffn': 1.0 + nrm(ks[10], (DEPTH, D_MODEL), 0.02),
        'norm_final': 1.0 + nrm(ks[11], (D_MODEL,), 0.02),
        'ab_w_in': nrm(ks[12], (N_ATTN_LAYERS, D_MODEL, D_IN_AB), D_MODEL ** -0.5),
        'ab_q_gain': 1.0 + nrm(ks[13], (N_ATTN_LAYERS, HEAD_DIM), 0.02),
        'ab_k_gain': 1.0 + nrm(ks[14], (N_ATTN_LAYERS, HEAD_DIM), 0.02),
        'ab_sg_gain': 1.0 + nrm(ks[15], (N_ATTN_LAYERS, N_SG_GROUPS, SG_GROUP_DIM), 0.02),
        'ab_w_s': nrm(ks[16], (N_ATTN_LAYERS, N_SG_GROUPS, CHUNK, CHUNK), CHUNK ** -0.5),
        'ab_b_s': 1.0 + nrm(ks[17], (N_ATTN_LAYERS, N_SG_GROUPS, CHUNK), 0.02),
        'ab_w_out': nrm(ks[18], (N_ATTN_LAYERS, D_ATTN + D_SG, D_MODEL), (D_ATTN + D_SG) ** -0.5),
        'rnn_w_in': nrm(ks[19], (N_RNN_LAYERS, D_MODEL, 2 * D_RNN), D_MODEL ** -0.5),
        'rnn_conv_w': nrm(ks[20], (N_RNN_LAYERS, CONV_W, D_RNN), CONV_W ** -0.5),
        'rnn_conv_b': nrm(ks[21], (N_RNN_LAYERS, D_RNN), 0.01),
        'rnn_w_gx': nrm(ks[23], (N_RNN_LAYERS, 2, N_RNN_BLOCKS, RNN_BLOCK, RNN_BLOCK), RNN_BLOCK ** -0.5),
        'rnn_b_gx': nrm(ks[24], (N_RNN_LAYERS, 2, D_RNN), 0.01),
        'rnn_w_ga': nrm(ks[25], (N_RNN_LAYERS, 2, N_RNN_BLOCKS, RNN_BLOCK, RNN_BLOCK), RNN_BLOCK ** -0.5),
        'rnn_b_ga': nrm(ks[26], (N_RNN_LAYERS, 2, D_RNN), 0.01),
        'rnn_lambda': lam,
        'rnn_w_out': nrm(ks[27], (N_RNN_LAYERS, D_RNN, D_MODEL), D_RNN ** -0.5),
        'moe_w_router': nrm(ks[28], (DEPTH, D_MODEL, N_EXPERTS), D_MODEL ** -0.5),
        'moe_w_gate': nrm(ks[29], (DEPTH, N_EXPERTS, D_MODEL, D_EXPERT), D_MODEL ** -0.5),
        'moe_w_up': nrm(ks[30], (DEPTH, N_EXPERTS, D_MODEL, D_EXPERT), D_MODEL ** -0.5),
        'moe_w_down': nrm(ks[31], (DEPTH, N_EXPERTS, D_EXPERT, D_MODEL), D_EXPERT ** -0.5),
    }


def reference(x_prompt, x_sample, c, cache_k, cache_v, state_rglru, c_ctx, w_mod, b_mod, norm_mix, norm_ffn,
              norm_final, ab_w_in, ab_q_gain, ab_k_gain, ab_sg_gain, ab_w_s, ab_b_s, ab_w_out, rnn_w_in,
              rnn_conv_w, rnn_conv_b, rnn_w_gx, rnn_b_gx, rnn_w_ga, rnn_b_ga, rnn_lambda, rnn_w_out,
              moe_w_router, moe_w_gate, moe_w_up, moe_w_down):
    xp = x_prompt
    xs = x_sample
    cos, sin = axial_rope_tables(xs.shape[1])
    cond_ctx = c_ctx[None, :]
    new_k, new_v, new_h = [], [], []
    for l in range(DEPTH):
        mp = modulation(cond_ctx, w_mod[l], b_mod[l])
        ms = modulation(c, w_mod[l], b_mod[l])
        hp = modulate(rms_norm(xp, norm_mix[l]), mp[0], mp[1])
        hs = modulate(rms_norm(xs, norm_mix[l]), ms[0], ms[1])
        i = l // 2
        if l % 2 == 0:
            op, kp, vp = ab_mixer_context(hp, ab_w_in[i], ab_q_gain[i], ab_k_gain[i], ab_sg_gain[i],
                                          ab_w_s[i], ab_b_s[i], ab_w_out[i])
            os_ = ab_mixer_latent(hs, cache_k[:, i], cache_v[:, i], cos, sin, ab_w_in[i], ab_q_gain[i],
                                  ab_k_gain[i], ab_sg_gain[i], ab_w_s[i], ab_b_s[i], ab_w_out[i])
            new_k.append(kp)
            new_v.append(vp)
        else:
            h0 = jnp.zeros((xp.shape[0], 2, D_RNN), F32)
            op, hfin = rglru_mixer(hp, h0, rnn_w_in[i], rnn_conv_w[i], rnn_conv_b[i], rnn_w_gx[i], rnn_b_gx[i],
                                   rnn_w_ga[i], rnn_b_ga[i], rnn_lambda[i], rnn_w_out[i], True)
            os_ = rglru_mixer(hs, state_rglru[:, i], rnn_w_in[i], rnn_conv_w[i], rnn_conv_b[i], rnn_w_gx[i],
                              rnn_b_gx[i], rnn_w_ga[i], rnn_b_ga[i], rnn_lambda[i], rnn_w_out[i], False)
            new_h.append(hfin.astype(xp.dtype))
        xp = xp + mp[2] * op
        xs = xs + ms[2] * os_
        hp = modulate(rms_norm(xp, norm_ffn[l]), mp[3], mp[4])
        hs = modulate(rms_norm(xs, norm_ffn[l]), ms[3], ms[4])
        xp = xp + mp[5] * expert_choice_moe(hp, moe_w_router[l], moe_w_gate[l], moe_w_up[l], moe_w_down[l])
        xs = xs + ms[5] * expert_choice_moe(hs, moe_w_router[l], moe_w_gate[l], moe_w_up[l], moe_w_down[l])
    y_prompt = rms_norm(xp, norm_final)
    y_sample = rms_norm(xs, norm_final)
    new_cache_k = jnp.stack(new_k, axis=1)
    new_cache_v = jnp.stack(new_v, axis=1)
    new_state_rglru = jnp.stack(new_h, axis=1)
    return (y_prompt, y_sample, new_cache_k, new_cache_v, new_state_rglru)
```

```python
import functools

import jax
import jax.numpy as jnp
from jax import lax
from jax.experimental import pallas as pl
from jax.experimental.pallas import tpu as pltpu

F32 = jnp.float32
BF16 = jnp.bfloat16

D = 2048
NB_P, N_P = 16, 256
NB_S, N_S = 2, 1024
TP = NB_P * N_P
TS = NB_S * N_S
T = TP + TS
PAST = 512
GRID_W = 64
HD = 128
NQ, NKV, GQA = 8, 2, 4
D_ATTN = NQ * HD
D_KV = NKV * HD
D_SG = D - D_ATTN
N_SG = 8
CHUNK = 128
D_IN_AB = D_ATTN + 2 * D_KV + 2 * D_SG
ROPE_THETA = 10000.0
D_RNN = D
RNN_BLK = 128
CONV_W, CONV_LEFT = 4, 2
LRU_C = 8.0
NE = 16
CAP_P = 2 * N_P // NE
CAP_S = 2 * N_S // NE
ROWS_P = NB_P * CAP_P
ROWS_S = NB_S * CAP_S
ROWS_E = ROWS_P + ROWS_S
D_EXP = 2048
EPS = 1e-6
N_COND = 8

LANES = 128
SUBLANES = 8
VMEM_LIMIT = 56 * 1024 * 1024

TM = 1024
TN = 512
ANY = pl.BlockSpec(memory_space=pl.ANY)


def _cp(n_axes, vmem=VMEM_LIMIT):
    return pltpu.CompilerParams(dimension_semantics=("arbitrary",) * n_axes, vmem_limit_bytes=vmem)


def _mod_row(tok0):
    return jnp.where(tok0 < TP, 0, 1 + lax.div(jnp.maximum(tok0 - TP, 0), N_S))


def _rms(x):
    return x * lax.rsqrt(jnp.mean(x * x, axis=-1, keepdims=True) + EPS)


def _mod_kernel(c_ref, w_ref, b_ref, o_ref):
    c = c_ref[...]
    s = (c * jax.nn.sigmoid(c)).astype(BF16)
    o_ref[...] = jnp.dot(s, w_ref[...].astype(BF16), preferred_element_type=F32) + b_ref[...]


def _modulation(cond, w_mod, b_mod):
    depth, _, n6 = w_mod.shape
    tn = 1024
    return pl.pallas_call(
        _mod_kernel,
        out_shape=jax.ShapeDtypeStruct((depth, N_COND, n6), F32),
        grid=(depth, n6 // tn),
        in_specs=[
            pl.BlockSpec((N_COND, D), lambda l, j: (0, 0)),
            pl.BlockSpec((None, D, tn), lambda l, j: (l, 0, j)),
            pl.BlockSpec((None, 1, tn), lambda l, j: (l, 0, j)),
        ],
        out_specs=pl.BlockSpec((None, N_COND, tn), lambda l, j: (l, 0, j)),
        compiler_params=_cp(2),
        name="modulation",
    )(cond, w_mod, b_mod.reshape(depth, 1, n6))


def _norm_mod_kernel(x_ref, g_ref, sh_ref, sc_ref, *rest, tm, with_router):
    r = _mod_row(pl.program_id(0) * tm)
    y = _rms(x_ref[...]) * g_ref[...]
    h = (y * (1 + sc_ref[pl.ds(r, 1), :]) + sh_ref[pl.ds(r, 1), :]).astype(BF16)
    if with_router:
        wr_ref, o_ref, lt_ref = rest
        lt_ref[...] = lax.dot_general(wr_ref[...].astype(BF16), h, (((1,), (1,)), ((), ())),
                                      preferred_element_type=F32)
    else:
        (o_ref,) = rest
    o_ref[...] = h


def _norm_mod(x, gain, m, chunk, w_router=None):
    tm = 512
    with_router = w_router is not None
    in_specs = [
        pl.BlockSpec((tm, D), lambda i: (i, 0)),
        pl.BlockSpec((1, D), lambda i: (0, 0)),
        pl.BlockSpec((N_COND, D), lambda i: (0, chunk)),
        pl.BlockSpec((N_COND, D), lambda i: (0, chunk + 1)),
    ]
    args = [x, gain.reshape(1, D), m, m]
    out_shape = [jax.ShapeDtypeStruct((T, D), BF16)]
    out_specs = [pl.BlockSpec((tm, D), lambda i: (i, 0))]
    if with_router:
        in_specs.append(pl.BlockSpec((NE, D), lambda i: (0, 0)))
        args.append(w_router.T)
        out_shape.append(jax.ShapeDtypeStruct((NE, T), F32))
        out_specs.append(pl.BlockSpec((NE, tm), lambda i: (0, i)))
    out = pl.pallas_call(
        functools.partial(_norm_mod_kernel, tm=tm, with_router=with_router),
        out_shape=out_shape,
        grid=(T // tm,),
        in_specs=in_specs,
        out_specs=out_specs,
        compiler_params=_cp(1),
        name="norm_mod_router" if with_router else "norm_mod",
    )(*args)
    return out if with_router else out[0]


def _final_norm_kernel(x_ref, g_ref, o_ref):
    o_ref[...] = _rms(x_ref[...]) * g_ref[...]


def _final_norm(x, gain, row0, rows):
    tm = 512
    blk0 = row0 // tm
    return pl.pallas_call(
        _final_norm_kernel,
        out_shape=jax.ShapeDtypeStruct((rows, D), F32),
        grid=(rows // tm,),
        in_specs=[pl.BlockSpec((tm, D), lambda i: (blk0 + i, 0)), pl.BlockSpec((1, D), lambda i: (0, 0))],
        out_specs=pl.BlockSpec((tm, D), lambda i: (i, 0)),
        compiler_params=_cp(1),
        name="final_norm",
    )(x, gain.reshape(1, D))


def _mm_kernel(*refs, n_a, epilogue):
    a_refs, w_ref, extras, o_ref, wbf = refs[:n_a], refs[n_a], refs[n_a + 1:-2], refs[-2], refs[-1]
    j, i = pl.program_id(0), pl.program_id(1)

    @pl.when(i == 0)
    def _():
        wbf[...] = w_ref[...].astype(BF16)

    acc, k0 = None, 0
    for a_ref in a_refs:
        ka = a_ref.shape[1]
        part = jnp.dot(a_ref[...], wbf[k0:k0 + ka, :], preferred_element_type=F32)
        acc = part if acc is None else acc + part
        k0 += ka
    epilogue(acc, j, i, extras, o_ref)


def _matmul(a_list, w, n_out, epilogue, extras, extra_specs, out_dtype, name):
    k = w.shape[0]
    in_specs = [pl.BlockSpec((TM, a.shape[1]), lambda j, i: (i, 0)) for a in a_list]
    in_specs.append(pl.BlockSpec((k, TN), lambda j, i: (0, j)))
    in_specs.extend(extra_specs)
    return pl.pallas_call(
        functools.partial(_mm_kernel, n_a=len(a_list), epilogue=epilogue),
        out_shape=jax.ShapeDtypeStruct((T, n_out), out_dtype),
        grid=(n_out // TN, T // TM),
        in_specs=in_specs,
        out_specs=pl.BlockSpec((TM, TN), lambda j, i: (i, j)),
        scratch_shapes=[pltpu.VMEM((k, TN), BF16)],
        compiler_params=_cp(2),
        name=name,
    )(*a_list, w, *extras)


def _ab_in_epilogue(acc, j, i, extras, o_ref):
    qg_ref, kg_ref, cos_ref, sin_ref = extras
    rope_on = i >= TP // TM

    def head(x, g_ref):
        y = _rms(x) * g_ref[...]
        lane = lax.broadcasted_iota(jnp.int32, y.shape, 1)
        partner = jnp.where(lane % 2 == 0, pltpu.roll(y, HD - 1, 1), pltpu.roll(y, 1, 1))
        return jnp.where(rope_on, y * cos_ref[...] + partner * sin_ref[...], y)

    @pl.when(j < D_ATTN // TN)
    def _():
        for hh in range(TN // HD):
            o_ref[:, hh * HD:(hh + 1) * HD] = head(acc[:, hh * HD:(hh + 1) * HD], qg_ref)

    @pl.when(j == D_ATTN // TN)
    def _():
        for hh in range(NKV):
            o_ref[:, hh * HD:(hh + 1) * HD] = head(acc[:, hh * HD:(hh + 1) * HD], kg_ref)
        o_ref[:, D_KV:] = acc[:, D_KV:]

    @pl.when(j > D_ATTN // TN)
    def _():
        o_ref[...] = jax.nn.gelu(acc)


def _rnn_in_epilogue(acc, j, i, extras, o_ref):
    @pl.when(j < D_RNN // TN)
    def _():
        o_ref[...] = acc

    @pl.when(j >= D_RNN // TN)
    def _():
        o_ref[...] = jax.nn.gelu(acc)


def _residual_epilogue(acc, j, i, extras, o_ref):
    x_ref, gate_ref = extras
    r = _mod_row(i * TM)
    o_ref[...] = x_ref[...] + gate_ref[pl.ds(r, 1), :] * acc


def _residual_matmul(a_list, w, x, m, chunk, name):
    extra_specs = [
        pl.BlockSpec((TM, TN), lambda j, i: (i, j)),
        pl.BlockSpec((N_COND, TN), lambda j, i: (0, chunk * (D // TN) + j)),
    ]
    return _matmul(a_list, w, D, _residual_epilogue, [x, m], extra_specs, F32, name)


def _attn_kernel(*refs, has_ctx):
    if has_ctx:
        q_ref, k_ref, v_ref, ck_ref, cv_ref, _, o_ref = refs
    else:
        q_ref, k_ref, v_ref, o_ref = refs
    scale = HD ** -0.5
    nt = (((1,), (1,)), ((), ()))
    k = k_ref[...].astype(BF16)
    v = v_ref[...].astype(BF16)
    if has_ctx:
        ck = ck_ref[...].astype(BF16)
        cv = cv_ref[...].astype(BF16)
    for g in range(GQA):
        q = q_ref[:, g * HD:(g + 1) * HD].astype(BF16)
        s = lax.dot_general(q, k, nt, preferred_element_type=F32) * scale
        mx = jnp.max(s, axis=-1, keepdims=True)
        if has_ctx:
            sc = lax.dot_general(q, ck, nt, preferred_element_type=F32) * scale
            mx = jnp.maximum(mx, jnp.max(sc, axis=-1, keepdims=True))
        p = jnp.exp(s - mx)
        den = jnp.sum(p, axis=-1, keepdims=True)
        o = jnp.dot(p.astype(BF16), v, preferred_element_type=F32)
        if has_ctx:
            pc = jnp.exp(sc - mx)
            den = den + jnp.sum(pc, axis=-1, keepdims=True)
            o = o + jnp.dot(pc.astype(BF16), cv, preferred_element_type=F32)
        o_ref[:, g * HD:(g + 1) * HD] = (o / den).astype(o_ref.dtype)


def _attention(proj, cache_k, cache_v):
    tq = 256
    kcol, vcol = D_ATTN // HD, (D_ATTN + D_KV) // HD
    out_shape = jax.ShapeDtypeStruct((T, D_ATTN), BF16)
    attn = pl.pallas_call(
        functools.partial(_attn_kernel, has_ctx=False),
        out_shape=out_shape,
        grid=(NB_P, NKV),
        in_specs=[
            pl.BlockSpec((N_P, GQA * HD), lambda b, kv: (b, kv)),
            pl.BlockSpec((N_P, HD), lambda b, kv: (b, kcol + kv)),
            pl.BlockSpec((N_P, HD), lambda b, kv: (b, vcol + kv)),
        ],
        out_specs=pl.BlockSpec((N_P, GQA * HD), lambda b, kv: (b, kv)),
        compiler_params=_cp(2),
        name="attn_prompt",
    )(proj, proj, proj)
    qb0, nqb = TP // tq, N_S // tq
    sb0 = TP // N_S
    return pl.pallas_call(
        functools.partial(_attn_kernel, has_ctx=True),
        out_shape=out_shape,
        grid=(NB_S, NKV, nqb),
        in_specs=[
            pl.BlockSpec((tq, GQA * HD), lambda b, kv, qi: (qb0 + b * nqb + qi, kv)),
            pl.BlockSpec((N_S, HD), lambda b, kv, qi: (sb0 + b, kcol + kv)),
            pl.BlockSpec((N_S, HD), lambda b, kv, qi: (sb0 + b, vcol + kv)),
            pl.BlockSpec((None, PAST, HD), lambda b, kv, qi: (b, 0, kv)),
            pl.BlockSpec((None, PAST, HD), lambda b, kv, qi: (b, 0, kv)),
            ANY,
        ],
        out_specs=pl.BlockSpec((tq, GQA * HD), lambda b, kv, qi: (qb0 + b * nqb + qi, kv)),
        input_output_aliases={5: 0},
        compiler_params=_cp(3),
        name="attn_sample",
    )(proj, proj, proj, cache_k, cache_v, attn)


def _sgu_kernel(u_ref, v_ref, gain_ref, ws_ref, bs_ref, o_ref, *, rows):
    ws = ws_ref[...].astype(BF16)
    bias = bs_ref[...]
    for c in range(rows // CHUNK):
        rs = slice(c * CHUNK, (c + 1) * CHUNK)
        vn = (_rms(v_ref[rs, :]) * gain_ref[...]).astype(BF16)
        mixed = jnp.dot(ws, vn, preferred_element_type=F32) + bias
        o_ref[rs, :] = (u_ref[rs, :] * mixed).astype(o_ref.dtype)


def _spatial_gating(proj, sg_gain, w_s, b_s):
    rows = 1024
    ucol = (D_ATTN + 2 * D_KV) // CHUNK
    vcol = ucol + D_SG // CHUNK
    return pl.pallas_call(
        functools.partial(_sgu_kernel, rows=rows),
        out_shape=jax.ShapeDtypeStruct((T, D_SG), BF16),
        grid=(T // rows, N_SG),
        in_specs=[
            pl.BlockSpec((rows, CHUNK), lambda i, g: (i, ucol + g)),
            pl.BlockSpec((rows, CHUNK), lambda i, g: (i, vcol + g)),
            pl.BlockSpec((None, 1, CHUNK), lambda i, g: (g, 0, 0)),
            pl.BlockSpec((None, CHUNK, CHUNK), lambda i, g: (g, 0, 0)),
            pl.BlockSpec((None, CHUNK, 1), lambda i, g: (g, 0, 0)),
        ],
        out_specs=pl.BlockSpec((rows, CHUNK), lambda i, g: (i, g)),
        compiler_params=_cp(2),
        name="spatial_gating",
    )(proj, proj, sg_gain.reshape(N_SG, 1, CHUNK), w_s, b_s.reshape(N_SG, CHUNK, 1))


def _softplus(x):
    return jnp.maximum(x, 0.0) + jnp.log1p(jnp.exp(-jnp.abs(x)))


def _rglru_kernel(*refs, n, tc, n_prev):
    xb_ref, gb_ref, cw_ref, cb_ref, wgx_ref, wga_ref, bgx_ref, bga_ref, lam_ref, h0_ref = refs[:10]
    y_ref, st_ref, xpad, a_f, b_f, a_b, b_b = refs[10 + n_prev:]
    pad = SUBLANES

    zeros = jnp.zeros((pad, tc), F32)
    xpad[0:pad, :] = zeros
    xpad[pad + n:pad + n + pad, :] = zeros
    xpad[pad:pad + n, :] = xb_ref[...]
    xc = xpad[pad - CONV_LEFT:pad - CONV_LEFT + n, :] * cw_ref[0:1, :]
    for t in range(1, CONV_W):
        xc = xc + xpad[pad - CONV_LEFT + t:pad - CONV_LEFT + t + n, :] * cw_ref[t:t + 1, :]
    xc = xc + cb_ref[...]

    sp = _softplus(-lam_ref[...])
    scans = ((a_f, b_f), (a_b, b_b))
    for kb in range(tc // RNN_BLK):
        cs = slice(kb * RNN_BLK, (kb + 1) * RNN_BLK)
        xk = xc[:, cs]
        w4 = jnp.concatenate([wgx_ref[0, kb], wgx_ref[1, kb], wga_ref[0, kb], wga_ref[1, kb]], axis=1)
        gg = jnp.dot(xk.astype(BF16), w4.astype(BF16), preferred_element_type=F32)
        for z in range(2):
            gx = jax.nn.sigmoid(gg[:, z * RNN_BLK:(z + 1) * RNN_BLK] + bgx_ref[z:z + 1, cs])
            ga = jax.nn.sigmoid(gg[:, (2 + z) * RNN_BLK:(3 + z) * RNN_BLK] + bga_ref[z:z + 1, cs])
            log_a = -LRU_C * ga * sp[z:z + 1, cs]
            scans[z][0][:, cs] = jnp.exp(log_a)
            scans[z][1][:, cs] = jnp.sqrt(1.0 - jnp.exp(2.0 * log_a)) * gx * xk

    row = lax.broadcasted_iota(jnp.int32, (SUBLANES, tc), 0)
    n_chunks = n // SUBLANES

    def chunk_scan(a_ref, b_ref, r0, carry, reverse):
        a = a_ref[pl.ds(r0, SUBLANES), :]
        b = b_ref[pl.ds(r0, SUBLANES), :]
        for s in (1, 2, 4):
            shift = SUBLANES - s if reverse else s
            keep = (row < SUBLANES - s) if reverse else (row >= s)
            b = jnp.where(keep, a * pltpu.roll(b, shift, 0) + b, b)
            a = jnp.where(keep, a * pltpu.roll(a, shift, 0), a)
        h = a * carry + b
        b_ref[pl.ds(r0, SUBLANES), :] = h
        return h[0:1, :] if reverse else h[SUBLANES - 1:SUBLANES, :]

    def step(c, carry):
        hf, hb = carry
        rf = pl.multiple_of(c * SUBLANES, SUBLANES)
        rb = pl.multiple_of((n_chunks - 1 - c) * SUBLANES, SUBLANES)
        return chunk_scan(a_f, b_f, rf, hf, False), chunk_scan(a_b, b_b, rb, hb, True)

    hf, hb = lax.fori_loop(0, n_chunks, step, (h0_ref[0:1, :], h0_ref[1:2, :]))
    st_ref[0:1, :] = hf
    st_ref[1:2, :] = hb
    y_ref[...] = ((b_f[...] + b_b[...]) * gb_ref[...]).astype(y_ref.dtype)


def _rglru_call(xz, params, h0, n, n_seq, row_blk0, prev_y, name):
    tc = 256
    conv_w, conv_b, w_gx, b_gx, w_ga, b_ga, lam = params
    gcol = D_RNN // tc
    kb = tc // RNN_BLK
    in_specs = [
        pl.BlockSpec((n, tc), lambda s, c: (row_blk0 + s, c)),
        pl.BlockSpec((n, tc), lambda s, c: (row_blk0 + s, gcol + c)),
        pl.BlockSpec((CONV_W, tc), lambda s, c: (0, c)),
        pl.BlockSpec((1, tc), lambda s, c: (0, c)),
        pl.BlockSpec((2, kb, RNN_BLK, RNN_BLK), lambda s, c: (0, c, 0, 0)),
        pl.BlockSpec((2, kb, RNN_BLK, RNN_BLK), lambda s, c: (0, c, 0, 0)),
        pl.BlockSpec((2, tc), lambda s, c: (0, c)),
        pl.BlockSpec((2, tc), lambda s, c: (0, c)),
        pl.BlockSpec((2, tc), lambda s, c: (0, c)),
        pl.BlockSpec((None, 2, tc), lambda s, c: (s, 0, c)),
    ]
    args = [xz, xz, conv_w, conv_b.reshape(1, D_RNN), w_gx, w_ga, b_gx, b_ga, lam, h0]
    aliases = {}
    if prev_y is not None:
        in_specs.append(ANY)
        args.append(prev_y)
        aliases = {10: 0}
    return pl.pallas_call(
        functools.partial(_rglru_kernel, n=n, tc=tc, n_prev=len(aliases)),
        out_shape=[jax.ShapeDtypeStruct((T, D_RNN), BF16), jax.ShapeDtypeStruct((n_seq, 2, D_RNN), F32)],
        grid=(n_seq, D_RNN // tc),
        in_specs=in_specs,
        out_specs=[pl.BlockSpec((n, tc), lambda s, c: (row_blk0 + s, c)),
                   pl.BlockSpec((None, 2, tc), lambda s, c: (s, 0, c))],
        scratch_shapes=[pltpu.VMEM((n + 2 * SUBLANES, tc), F32)] + [pltpu.VMEM((n, tc), F32)] * 4,
        input_output_aliases=aliases,
        compiler_params=_cp(2),
        name=name,
    )(*args)


def _route_kernel(lt_ref, *rest, n, cap, n_prev):
    slot_ref, aff_ref = rest[n_prev:]
    lt = lt_ref[...]
    ex = jnp.exp(lt - jnp.max(lt, axis=0, keepdims=True))
    aff = ex / jnp.sum(ex, axis=0, keepdims=True)
    aff_ref[...] = aff
    aff_t = jnp.concatenate([aff, jnp.zeros((LANES - NE, n), F32)], axis=0).T
    blk = 256
    ranks = []
    for e in range(NE):
        mine = aff[e:e + 1, :]
        rank = jnp.zeros((1, n), F32)
        for r0 in range(0, n, blk):
            other = aff_t[r0:r0 + blk, e:e + 1]
            t_other = lax.broadcasted_iota(jnp.int32, (blk, n), 0) + r0
            t_mine = lax.broadcasted_iota(jnp.int32, (blk, n), 1)
            ahead = (other > mine) | ((other == mine) & (t_other < t_mine))
            rank = rank + jnp.sum(ahead.astype(F32), axis=0, keepdims=True)
        ranks.append(rank)
    rank = jnp.concatenate(ranks, axis=0)
    slot_ref[...] = jnp.where(rank < cap, rank, -1.0)


def _route(lt, n, n_seq, blk0, cap, prev, name):
    in_specs = [pl.BlockSpec((NE, n), lambda b: (0, blk0 + b))]
    args = [lt]
    aliases = {}
    if prev is not None:
        in_specs.extend([ANY, ANY])
        args.extend(prev)
        aliases = {1: 0, 2: 1}
    return pl.pallas_call(
        functools.partial(_route_kernel, n=n, cap=cap, n_prev=len(aliases)),
        out_shape=[jax.ShapeDtypeStruct((NE, T), F32)] * 2,
        grid=(n_seq,),
        in_specs=in_specs,
        out_specs=[pl.BlockSpec((NE, n), lambda b: (0, blk0 + b))] * 2,
        input_output_aliases=aliases,
        compiler_params=_cp(1),
        name=name,
    )(*args)


def _one_hot_slots(slot, n_exp, cap, n):
    rows = n_exp * cap
    slot_rows = jnp.broadcast_to(slot[:, None, :], (n_exp, cap, n)).reshape(rows, n)
    slot_id = (lax.broadcasted_iota(jnp.int32, (rows, n), 0) % cap).astype(F32)
    return slot_rows == slot_id


def _gather_kernel(slot_ref, aff_ref, h_ref, xs_ref, g_ref, *, n, cap, eg):
    h = h_ref[...]
    rows = eg * cap
    for e0 in range(0, NE, eg):
        pick = _one_hot_slots(slot_ref[e0:e0 + eg, :], eg, cap, n)
        x = jnp.dot(pick.astype(F32).astype(BF16), h, preferred_element_type=F32)
        xs_ref[e0:e0 + eg] = x.astype(BF16).reshape(eg, cap, D)
        aff_rows = jnp.broadcast_to(aff_ref[e0:e0 + eg, :][:, None, :], (eg, cap, n)).reshape(rows, n)
        gate = jnp.sum(jnp.where(pick, aff_rows, 0.0), axis=1, keepdims=True)
        g_ref[e0:e0 + eg] = jnp.broadcast_to(gate, (rows, LANES)).reshape(eg, cap, LANES)


def _gather(slot, aff, h, n, n_seq, blk0, cap, name):
    eg = 512 // cap
    return pl.pallas_call(
        functools.partial(_gather_kernel, n=n, cap=cap, eg=eg),
        out_shape=[jax.ShapeDtypeStruct((NE, n_seq * cap, D), BF16),
                   jax.ShapeDtypeStruct((NE, n_seq * cap, LANES), F32)],
        grid=(n_seq,),
        in_specs=[
            pl.BlockSpec((NE, n), lambda b: (0, blk0 + b)),
            pl.BlockSpec((NE, n), lambda b: (0, blk0 + b)),
            pl.BlockSpec((n, D), lambda b: (blk0 + b, 0)),
        ],
        out_specs=[pl.BlockSpec((NE, cap, D), lambda b: (0, b, 0)),
                   pl.BlockSpec((NE, cap, LANES), lambda b: (0, b, 0))],
        compiler_params=_cp(1),
        name=name,
    )(slot, aff, h)


def _ffn_kernel(xp_ref, xs_ref, gp_ref, gs_ref, wg_ref, wu_ref, wd_ref, y_ref, x_all, acc, *, n_f):
    f = pl.program_id(1)

    @pl.when(f == 0)
    def _():
        x_all[0:ROWS_P, :] = xp_ref[...]
        x_all[ROWS_P:ROWS_E, :] = xs_ref[...]
        acc[...] = jnp.zeros_like(acc)

    x = x_all[...]
    g = jnp.dot(x, wg_ref[...].astype(BF16), preferred_element_type=F32)
    u = jnp.dot(x, wu_ref[...].astype(BF16), preferred_element_type=F32)
    hid = (g * jax.nn.sigmoid(g) * u).astype(BF16)
    acc[...] += jnp.dot(hid, wd_ref[...].astype(BF16), preferred_element_type=F32)

    @pl.when(f == n_f - 1)
    def _():
        for c0 in range(0, D, LANES):
            y_ref[0:ROWS_P, c0:c0 + LANES] = (acc[0:ROWS_P, c0:c0 + LANES] * gp_ref[...]).astype(BF16)
            y_ref[ROWS_P:ROWS_E, c0:c0 + LANES] = (acc[ROWS_P:ROWS_E, c0:c0 + LANES] * gs_ref[...]).astype(BF16)


def _expert_ffn(xs_p, xs_s, g_p, g_s, w_gate, w_up, w_down, layer):
    tf = 256
    n_f = D_EXP // tf
    return pl.pallas_call(
        functools.partial(_ffn_kernel, n_f=n_f),
        out_shape=jax.ShapeDtypeStruct((NE, ROWS_E, D), BF16),
        grid=(NE, n_f),
        in_specs=[
            pl.BlockSpec((None, ROWS_P, D), lambda e, f: (e, 0, 0)),
            pl.BlockSpec((None, ROWS_S, D), lambda e, f: (e, 0, 0)),
            pl.BlockSpec((None, ROWS_P, LANES), lambda e, f: (e, 0, 0)),
            pl.BlockSpec((None, ROWS_S, LANES), lambda e, f: (e, 0, 0)),
            pl.BlockSpec((None, None, D, tf), lambda e, f: (layer, e, 0, f)),
            pl.BlockSpec((None, None, D, tf), lambda e, f: (layer, e, 0, f)),
            pl.BlockSpec((None, None, tf, D), lambda e, f: (layer, e, f, 0)),
        ],
        out_specs=pl.BlockSpec((None, ROWS_E, D), lambda e, f: (e, 0, 0)),
        scratch_shapes=[pltpu.VMEM((ROWS_E, D), BF16), pltpu.VMEM((ROWS_E, D), F32)],
        compiler_params=_cp(2),
        name="expert_ffn",
    )(xs_p, xs_s, g_p, g_s, w_gate, w_up, w_down)


def _scatter_kernel(slot_ref, y_ref, x_ref, gate_ref, *rest, n, cap, tc, row0, row_step):
    o_ref = rest[-1]
    pick = _one_hot_slots(slot_ref[...], NE, cap, n).astype(F32).astype(BF16)
    y = y_ref[...].reshape(NE * cap, tc)
    moe = lax.dot_general(pick, y, (((0,), (0,)), ((), ())), preferred_element_type=F32)
    r = row0 + row_step * pl.program_id(0)
    o_ref[...] = x_ref[...] + gate_ref[pl.ds(r, 1), :] * moe


def _scatter(slot, ys, x, m, chunk, n, n_seq, blk0, cap, yblk0, row0, row_step, prev, name):
    tc = 512
    in_specs = [
        pl.BlockSpec((NE, n), lambda b, c: (0, blk0 + b)),
        pl.BlockSpec((NE, cap, tc), lambda b, c: (0, yblk0 + b, c)),
        pl.BlockSpec((n, tc), lambda b, c: (blk0 + b, c)),
        pl.BlockSpec((N_COND, tc), lambda b, c: (0, chunk * (D // tc) + c)),
    ]
    args = [slot, ys, x, m]
    aliases = {}
    if prev is not None:
        in_specs.append(ANY)
        args.append(prev)
        aliases = {4: 0}
    return pl.pallas_call(
        functools.partial(_scatter_kernel, n=n, cap=cap, tc=tc, row0=row0, row_step=row_step),
        out_shape=jax.ShapeDtypeStruct((T, D), F32),
        grid=(n_seq, D // tc),
        in_specs=in_specs,
        out_specs=pl.BlockSpec((n, tc), lambda b, c: (blk0 + b, c)),
        input_output_aliases=aliases,
        compiler_params=_cp(2),
        name=name,
    )(*args)


def _moe_block(x, m, norm_gain, w_router, w_gate, w_up, w_down, layer):
    h, lt = _norm_mod(x, norm_gain, m, 3, w_router)
    sb0 = TP // N_S
    routed = _route(lt, N_P, NB_P, 0, CAP_P, None, "route_prompt")
    slot, aff = _route(lt, N_S, NB_S, sb0, CAP_S, routed, "route_sample")
    xs_p, g_p = _gather(slot, aff, h, N_P, NB_P, 0, CAP_P, "gather_prompt")
    xs_s, g_s = _gather(slot, aff, h, N_S, NB_S, sb0, CAP_S, "gather_sample")
    ys = _expert_ffn(xs_p, xs_s, g_p, g_s, w_gate, w_up, w_down, layer)
    out = _scatter(slot, ys, x, m, 5, N_P, NB_P, 0, CAP_P, 0, 0, 0, None, "scatter_prompt")
    return _scatter(slot, ys, x, m, 5, N_S, NB_S, sb0, CAP_S, ROWS_P // CAP_S, 1, 1, out, "scatter_sample")


def _rope_tables():
    n_rows = N_S // GRID_W
    rows = jnp.repeat(jnp.arange(n_rows), GRID_W).astype(F32)
    cols = jnp.tile(jnp.arange(GRID_W), n_rows).astype(F32)
    half = HD // 2
    inv = ROPE_THETA ** (-jnp.arange(0, half, 2, dtype=F32) / half)
    ang = jnp.concatenate([rows[:, None] * inv, cols[:, None] * inv], axis=-1)
    sign = jnp.where(jnp.arange(HD) % 2 == 0, -1.0, 1.0).astype(F32)
    return jnp.repeat(jnp.cos(ang), 2, axis=-1), jnp.repeat(jnp.sin(ang), 2, axis=-1) * sign


def kernel(x_prompt, x_sample, c, cache_k, cache_v, state_rglru, c_ctx, w_mod, b_mod, norm_mix, norm_ffn, norm_final, ab_w_in, ab_q_gain, ab_k_gain, ab_sg_gain, ab_w_s, ab_b_s, ab_w_out, rnn_w_in, rnn_conv_w, rnn_conv_b, rnn_w_gx, rnn_b_gx, rnn_w_ga, rnn_b_ga, rnn_lambda, rnn_w_out, moe_w_router, moe_w_gate, moe_w_up, moe_w_down):
    assert x_prompt.shape == (NB_P, N_P, D) and x_sample.shape == (NB_S, N_S, D)
    x = jnp.concatenate([x_prompt.reshape(TP, D), x_sample.reshape(TS, D)], axis=0)
    cond = jnp.concatenate([c_ctx[None, :], c, jnp.zeros((N_COND - 1 - NB_S, D), F32)], axis=0)
    mod = _modulation(cond, w_mod, b_mod)
    cos, sin = _rope_tables()
    const = lambda j, i: (0, 0)

    m = mod[0]
    h = _norm_mod(x, norm_mix[0], m, 0)
    ab_specs = [pl.BlockSpec((1, HD), const), pl.BlockSpec((1, HD), const),
                pl.BlockSpec((N_S, HD), const), pl.BlockSpec((N_S, HD), const)]
    proj = _matmul([h], ab_w_in[0], D_IN_AB, _ab_in_epilogue,
                   [ab_q_gain[0].reshape(1, HD), ab_k_gain[0].reshape(1, HD), cos, sin], ab_specs, F32, "ab_in_proj")
    attn = _attention(proj, cache_k[:, 0].reshape(NB_S, PAST, D_KV), cache_v[:, 0].reshape(NB_S, PAST, D_KV))
    sg = _spatial_gating(proj, ab_sg_gain[0], ab_w_s[0], ab_b_s[0])
    x = _residual_matmul([attn, sg], ab_w_out[0], x, m, 2, "ab_out_proj")
    x = _moe_block(x, m, norm_ffn[0], moe_w_router[0], moe_w_gate, moe_w_up, moe_w_down, 0)
    new_k = proj[:TP, D_ATTN:D_ATTN + D_KV].reshape(NB_P, 1, N_P, NKV, HD)
    new_v = proj[:TP, D_ATTN + D_KV:D_ATTN + 2 * D_KV].reshape(NB_P, 1, N_P, NKV, HD)

    m = mod[1]
    h = _norm_mod(x, norm_mix[1], m, 0)
    xz = _matmul([h], rnn_w_in[0], 2 * D_RNN, _rnn_in_epilogue, [], [], F32, "rnn_in_proj")
    rnn = (rnn_conv_w[0], rnn_conv_b[0], rnn_w_gx[0], rnn_b_gx[0], rnn_w_ga[0], rnn_b_ga[0], rnn_lambda[0])
    y, state = _rglru_call(xz, rnn, jnp.zeros((NB_P, 2, D_RNN), F32), N_P, NB_P, 0, None, "rglru_prompt")
    y, _ = _rglru_call(xz, rnn, state_rglru[:, 0], N_S, NB_S, TP // N_S, y, "rglru_sample")
    x = _residual_matmul([y], rnn_w_out[0], x, m, 2, "rnn_out_proj")
    x = _moe_block(x, m, norm_ffn[1], moe_w_router[1], moe_w_gate, moe_w_up, moe_w_down, 1)

    y_prompt = _final_norm(x, norm_final, 0, TP).reshape(NB_P, N_P, D)
    y_sample = _final_norm(x, norm_final, TP, TS).reshape(NB_S, N_S, D)
    return (y_prompt, y_sample, new_k, new_v, state[:, None])
```
